```python
import math
import jax
import jax.numpy as jnp
from jax import lax
import numpy as np

D_MODEL = 1024
BATCH = 32
SEQ = 2048
DEPTH = 1

HEAD_DIM = 64
NA_HEADS = 8
NA_WIDTH = NA_HEADS * HEAD_DIM
DIFF_HEADS = 4
DIFF_QK_WIDTH = DIFF_HEADS * 2 * HEAD_DIM
DIFF_V_DIM = 2 * HEAD_DIM
DIFF_WIDTH = DIFF_HEADS * DIFF_V_DIM
IN_COLS = 3 * NA_WIDTH + 2 * DIFF_QK_WIDTH + DIFF_WIDTH
N_BRANCH = 2
GRID_W = 64
WIN_ROWS = 8
WIN_COLS = 16
Q_BLOCK = 128
D_FF = 2816
CONV_W = 3
N_MOD = 6
EPS = 1e-6

kernel_name = "hybrid_natten_diffattn_convffn_block"


def rms_norm(x, gain):
    xf = x.astype(jnp.float32)
    y = xf * lax.rsqrt(jnp.mean(xf * xf, axis=-1, keepdims=True) + EPS)
    return (y * gain.astype(jnp.float32)).astype(x.dtype)


def alibi_slopes(n_heads):
    return jnp.asarray([2.0 ** (-8.0 * (h + 1) / n_heads) for h in range(n_heads)], dtype=jnp.float32)


def lambda_init_for(layer_idx):
    return 0.8 - 0.6 * math.exp(-0.3 * layer_idx)


def neighborhood_attention(q, k, v, rpb):
    b, s, h, dh = q.shape
    rows = s // GRID_W
    kr = min(WIN_ROWS, rows)
    kc = WIN_COLS
    qg = q.reshape(b, rows, GRID_W, h, dh)
    kg = k.reshape(b, rows, GRID_W, h, dh)
    vg = v.reshape(b, rows, GRID_W, h, dh)
    col = jnp.arange(GRID_W)
    col_idx = jnp.clip(col - kc // 2, 0, GRID_W - kc)[:, None] + jnp.arange(kc)[None, :]
    dc = col_idx - col[:, None] + (WIN_COLS - 1)
    r_idx = jnp.arange(rows)
    row_start = jnp.clip(r_idx - kr // 2, 0, rows - kr)
    scale = dh ** -0.5

    def one_row(args):
        q_row, r, r0 = args
        k_band = lax.dynamic_slice_in_dim(kg, r0, kr, axis=1)
        v_band = lax.dynamic_slice_in_dim(vg, r0, kr, axis=1)
        k_nb = k_band[:, :, col_idx]
        v_nb = v_band[:, :, col_idx]
        logits = jnp.einsum('bchd,brckhd->bhcrk', q_row, k_nb,
                            preferred_element_type=jnp.float32) * scale
        dr = r0 + jnp.arange(kr) - r + (WIN_ROWS - 1)
        bias = rpb[:, dr[None, :, None], dc[:, None, :]]
        logits = logits + bias.astype(jnp.float32)[None]
        p = jax.nn.softmax(logits.reshape(b, h, GRID_W, kr * kc), axis=-1)
        p = p.reshape(b, h, GRID_W, kr, kc).astype(v.dtype)
        return jnp.einsum('bhcrk,brckhd->bchd', p, v_nb)

    out = lax.map(one_row, (jnp.moveaxis(qg, 1, 0), r_idx, row_start))
    return jnp.moveaxis(out, 0, 1).reshape(b, s, h * dh)


def differential_attention(q, k, v, lam, lambda_init, subln_gain):
    b, s, h, _, dh = q.shape
    nb = s // Q_BLOCK
    slopes = alibi_slopes(h)
    pos = jnp.arange(s, dtype=jnp.float32)
    scale = dh ** -0.5
    qb = jnp.moveaxis(q.reshape(b, nb, Q_BLOCK, h, 2, dh), 1, 0)
    qpos = pos.reshape(nb, Q_BLOCK)

    def one_block(args):
        q_blk, tq = args
        logits = jnp.einsum('bqhcd,bkhcd->bhcqk', q_blk, k,
                            preferred_element_type=jnp.float32) * scale
        bias = -slopes[:, None, None] * jnp.abs(tq[:, None] - pos[None, :])
        p = jax.nn.softmax(logits + bias[None, :, None], axis=-1)
        a = p[:, :, 0] - lam * p[:, :, 1]
        return jnp.einsum('bhqk,bkhd->bqhd', a.astype(v.dtype), v)

    out = lax.map(one_block, (qb, qpos))
    out = jnp.moveaxis(out, 0, 1).reshape(b, s, h, -1)
    out = rms_norm(out, subln_gain) * (1.0 - lambda_init)
    return out.reshape(b, s, -1)


def depthwise_conv_centered(u, w, bias):
    pad = CONV_W // 2
    s = u.shape[1]
    up = jnp.pad(u, ((0, 0), (pad, pad), (0, 0)))
    out = up[:, 0:s] * w[0]
    for i in range(1, CONV_W):
        out = out + up[:, i:i + s] * w[i]
    return out + bias


def setup_inputs(seed: int = 0) -> dict:
    key = jax.random.key(seed)
    ks = jax.random.split(key, 32)
    D = D_MODEL
    nrm = lambda k, shape, s: jax.random.normal(k, shape, dtype=jnp.float32) * s
    gain = lambda k, shape: 1.0 + nrm(k, shape, 0.02)
    return {
        "x": nrm(ks[0], (BATCH, SEQ, D), 1.0),
        "c": nrm(ks[1], (BATCH, D), 1.0),
        "ada_w": nrm(ks[2], (DEPTH, D, N_MOD * D), 0.5 * D ** -0.5),
        "ada_b": nrm(ks[3], (DEPTH, N_MOD * D), 0.02),
        "norm1_g": gain(ks[4], (DEPTH, D)),
        "w_in": nrm(ks[5], (DEPTH, D, IN_COLS), D ** -0.5),
        "na_q_g": gain(ks[6], (DEPTH, HEAD_DIM)),
        "na_k_g": gain(ks[7], (DEPTH, HEAD_DIM)),
        "na_rpb": nrm(ks[8], (DEPTH, NA_HEADS, 2 * WIN_ROWS - 1, 2 * WIN_COLS - 1), 0.1),
        "df_q_g": gain(ks[9], (DEPTH, HEAD_DIM)),
        "df_k_g": gain(ks[10], (DEPTH, HEAD_DIM)),
        "lam_q1": nrm(ks[11], (DEPTH, HEAD_DIM), 0.1),
        "lam_k1": nrm(ks[12], (DEPTH, HEAD_DIM), 0.1),
        "lam_q2": nrm(ks[13], (DEPTH, HEAD_DIM), 0.1),
        "lam_k2": nrm(ks[14], (DEPTH, HEAD_DIM), 0.1),
        "df_subln_g": gain(ks[15], (DEPTH, DIFF_V_DIM)),
        "w_na_proj": nrm(ks[16], (DEPTH, NA_WIDTH, D), NA_WIDTH ** -0.5),
        "w_df_proj": nrm(ks[17], (DEPTH, DIFF_WIDTH, D), DIFF_WIDTH ** -0.5),
        "w_gate": nrm(ks[18], (DEPTH, D, N_BRANCH * D), D ** -0.5),
        "b_gate": nrm(ks[19], (DEPTH, N_BRANCH * D), 0.02),
        "w_out": nrm(ks[20], (DEPTH, D, D), D ** -0.5),
        "norm2_g": gain(ks[21], (DEPTH, D)),
        "w_up": nrm(ks[22], (DEPTH, D, 2 * D_FF), D ** -0.5),
        "conv_w": nrm(ks[23], (DEPTH, CONV_W, 2 * D_FF), CONV_W ** -0.5),
        "conv_b": nrm(ks[24], (DEPTH, 2 * D_FF), 0.01),
        "w_down": nrm(ks[25], (DEPTH, D_FF, D), D_FF ** -0.5),
    }


def reference(x, c, ada_w, ada_b, norm1_g, w_in, na_q_g, na_k_g, na_rpb, df_q_g, df_k_g,
              lam_q1, lam_k1, lam_q2, lam_k2, df_subln_g, w_na_proj, w_df_proj, w_gate, b_gate,
              w_out, norm2_g, w_up, conv_w, conv_b, w_down):
    b, s, _ = x.shape
    splits = [NA_WIDTH, 2 * NA_WIDTH, 3 * NA_WIDTH, 3 * NA_WIDTH + DIFF_QK_WIDTH,
              3 * NA_WIDTH + 2 * DIFF_QK_WIDTH]
    c_act = jax.nn.silu(c)
    for l in range(DEPTH):
        lam_init = lambda_init_for(l)
        mod = (jnp.einsum('bd,de->be', c_act, ada_w[l]) + ada_b[l])[:, None, :]
        sh1, sc1, g1, sh2, sc2, g2 = jnp.split(mod, N_MOD, axis=-1)

        h = rms_norm(x, norm1_g[l]) * (1.0 + sc1) + sh1
        proj = jnp.einsum('bsd,de->bse', h, w_in[l])
        na_q, na_k, na_v, df_q, df_k, df_v = jnp.split(proj, splits, axis=-1)

        na_q = rms_norm(na_q.reshape(b, s, NA_HEADS, HEAD_DIM), na_q_g[l])
        na_k = rms_norm(na_k.reshape(b, s, NA_HEADS, HEAD_DIM), na_k_g[l])
        na_v = na_v.reshape(b, s, NA_HEADS, HEAD_DIM)
        y_na = neighborhood_attention(na_q, na_k, na_v, na_rpb[l])

        df_q = rms_norm(df_q.reshape(b, s, DIFF_HEADS, 2, HEAD_DIM), df_q_g[l])
        df_k = rms_norm(df_k.reshape(b, s, DIFF_HEADS, 2, HEAD_DIM), df_k_g[l])
        df_v = df_v.reshape(b, s, DIFF_HEADS, DIFF_V_DIM)
        lam = (jnp.exp(jnp.sum(lam_q1[l].astype(jnp.float32) * lam_k1[l].astype(jnp.float32)))
               - jnp.exp(jnp.sum(lam_q2[l].astype(jnp.float32) * lam_k2[l].astype(jnp.float32)))
               + lam_init)
        y_df = differential_attention(df_q, df_k, df_v, lam, lam_init, df_subln_g[l])

        ya = jnp.einsum('bse,ed->bsd', y_na, w_na_proj[l])
        yb = jnp.einsum('bse,ed->bsd', y_df, w_df_proj[l])
        gates = jax.nn.sigmoid(jnp.einsum('bsd,de->bse', h, w_gate[l]) + b_gate[l])
        ga, gb = jnp.split(gates, N_BRANCH, axis=-1)
        mixed = jnp.einsum('bsd,de->bse', ga * ya + gb * yb, w_out[l])
        x = x + g1 * mixed

        h2 = rms_norm(x, norm2_g[l]) * (1.0 + sc2) + sh2
        u = jnp.einsum('bsd,df->bsf', h2, w_up[l])
        u = depthwise_conv_centered(u, conv_w[l], conv_b[l])
        u_act, u_lin = jnp.split(u, 2, axis=-1)
        f = jnp.einsum('bsf,fd->bsd', jax.nn.gelu(u_act, approximate=False) * u_lin, w_down[l])
        x = x + g2 * f
    return x
```

```python
import functools
import math

import jax
import jax.numpy as jnp
from jax import lax
from jax.experimental import pallas as pl
from jax.experimental.pallas import tpu as pltpu

HEAD_DIM = 64
NA_HEADS = 8
DIFF_HEADS = 4
GRID_W = 64
WIN_ROWS = 8
WIN_COLS = 16
CONV_W = 3
N_MOD = 6
EPS = 1e-6

LANES = 128
GROUP = 512
MASK_VALUE = -1e30
VMEM_LIMIT_BYTES = 56 * 1024 * 1024

PROJ_TILE = 512
ATTN_TILE = 256
FFN_CHUNK = 256

BF16 = jnp.bfloat16
F32 = jnp.float32


def _dot(a, b):
    return jnp.dot(a, b, preferred_element_type=F32)


def _dot_nt(a, b):
    return lax.dot_general(a, b, (((1,), (1,)), ((), ())), preferred_element_type=F32)


def _rms_scale(xf):
    return lax.rsqrt(jnp.mean(xf * xf, axis=-1, keepdims=True) + EPS)


def _adaln_kernel(c_ref, w_ref, b_ref, o_ref):
    o_ref[...] = _dot(jax.nn.silu(c_ref[...]), w_ref[...]) + b_ref[...]


def _adaln(c, ada_w, ada_b):
    b, d = c.shape
    n = ada_w.shape[1]
    return pl.pallas_call(
        _adaln_kernel,
        grid=(n // d,),
        in_specs=[
            pl.BlockSpec((b, d), lambda j: (0, 0)),
            pl.BlockSpec((d, d), lambda j: (0, j)),
            pl.BlockSpec((1, d), lambda j: (0, j)),
        ],
        out_specs=pl.BlockSpec((b, d), lambda j: (0, j)),
        out_shape=jax.ShapeDtypeStruct((b, n), F32),
        compiler_params=pltpu.CompilerParams(dimension_semantics=("arbitrary",)),
        name="adaln",
    )(c, ada_w, ada_b.reshape(1, n))


def _proj_kernel(x_ref, mod_ref, n1g_ref, win_ref, wg_ref, bg_ref, qkg_ref, gsum_ref,
                 proj_ref, gates_ref):
    x = x_ref[0]
    sh1 = mod_ref[0, 0:1, :]
    sc1 = mod_ref[0, 1:2, :]
    h = x * _rms_scale(x) * n1g_ref[...] * (1.0 + sc1) + sh1
    hb = h.astype(BF16)
    normed = {0: 0, 1: 1, 3: 2, 4: 3}
    for j in range(6):
        p = _dot(hb, win_ref[:, j * GROUP:(j + 1) * GROUP])
        if j in normed:
            ms = _dot((p * p).astype(BF16), gsum_ref[...]) * (1.0 / HEAD_DIM)
            g = qkg_ref[normed[j]:normed[j] + 1, :]
            p = p * lax.rsqrt(ms + EPS) * g
        proj_ref[0, :, j * GROUP:(j + 1) * GROUP] = p.astype(BF16)
    g = _dot(hb, wg_ref[...]) + bg_ref[...]
    gates_ref[0] = jax.nn.sigmoid(g).astype(BF16)


def _proj(x, mod, norm1_g, w_in, w_gate, b_gate, qk_gain):
    b, s, d = x.shape
    n_in = w_in.shape[1]
    n_g = w_gate.shape[1]
    tm = PROJ_TILE
    head = jnp.arange(GROUP) // HEAD_DIM
    gsum = (head[:, None] == head[None, :]).astype(BF16)
    const = lambda bi, i: (0, 0)
    return pl.pallas_call(
        _proj_kernel,
        grid=(b, s // tm),
        in_specs=[
            pl.BlockSpec((1, tm, d), lambda bi, i: (bi, i, 0)),
            pl.BlockSpec((1, N_MOD, d), lambda bi, i: (bi, 0, 0)),
            pl.BlockSpec((1, d), const),
            pl.BlockSpec((d, n_in), const),
            pl.BlockSpec((d, n_g), const),
            pl.BlockSpec((1, n_g), const),
            pl.BlockSpec((4, GROUP), const),
            pl.BlockSpec((GROUP, GROUP), const),
        ],
        out_specs=[
            pl.BlockSpec((1, tm, n_in), lambda bi, i: (bi, i, 0)),
            pl.BlockSpec((1, tm, n_g), lambda bi, i: (bi, i, 0)),
        ],
        out_shape=[
            jax.ShapeDtypeStruct((b, s, n_in), BF16),
            jax.ShapeDtypeStruct((b, s, n_g), BF16),
        ],
        compiler_params=pltpu.CompilerParams(
            dimension_semantics=("parallel", "parallel"),
            vmem_limit_bytes=VMEM_LIMIT_BYTES),
        name="proj",
    )(x, mod, norm1_g.reshape(1, d), w_in.astype(BF16), w_gate.astype(BF16),
      b_gate.reshape(1, n_g), qk_gain, gsum)


def _softmax_rows(s):
    m = jnp.max(s, axis=-1, keepdims=True)
    p = jnp.exp(s - m)
    return p, 1.0 / jnp.sum(p, axis=-1, keepdims=True)


def _attn_kernel(qna_ref, kna_ref, vna_ref, qdf_ref, kdf_ref, vdf_ref, gates_ref, x_ref,
                 mod_ref, tab0_ref, tab1_ref, tab2_ref, tab3_ref, lamv_ref, subg_ref,
                 wna_ref, wdf_ref, wout_ref, o_ref, yna_ref, ydf_ref,
                 *, lam_init, slopes, rows_total):
    i = pl.program_id(1)
    tq = qna_ref.shape[1]
    s_len = kdf_ref.shape[1]
    lane = lax.broadcasted_iota(jnp.int32, (1, LANES), 1)
    low_half = lane < HEAD_DIM

    n_keys = WIN_ROWS * GRID_W
    tabs = (tab0_ref, tab1_ref, tab2_ref, tab3_ref)
    for a in range(tq // GRID_W):
        r = i * (tq // GRID_W) + a
        r0 = jnp.clip(r - WIN_ROWS // 2, 0, rows_total - WIN_ROWS)
        k_start = pl.multiple_of(r0 * GRID_W, GRID_W)
        for hp in range(NA_HEADS // 2):
            cols = slice(hp * LANES, (hp + 1) * LANES)
            q2 = qna_ref[0, a * GRID_W:(a + 1) * GRID_W, cols]
            k2 = kna_ref[0, pl.ds(k_start, n_keys), cols]
            v2 = vna_ref[0, pl.ds(k_start, n_keys), cols]
            outs = []
            for half in range(2):
                sel = low_half if half == 0 else jnp.logical_not(low_half)
                qh = jnp.where(sel, q2, jnp.zeros_like(q2))
                s = _dot_nt(qh, k2) + tabs[a][0, 2 * hp + half]
                p, inv_l = _softmax_rows(s)
                outs.append(_dot(p.astype(BF16), v2) * inv_l)
            y = jnp.where(low_half, outs[0], outs[1])
            yna_ref[a * GRID_W:(a + 1) * GRID_W, cols] = y.astype(BF16)

    lam_t1 = jnp.sum(lamv_ref[0:1, :] * lamv_ref[1:2, :], axis=-1, keepdims=True)
    lam_t2 = jnp.sum(lamv_ref[2:3, :] * lamv_ref[3:4, :], axis=-1, keepdims=True)
    lam = jnp.exp(lam_t1) - jnp.exp(lam_t2) + lam_init
    tq_pos = (i * tq + lax.broadcasted_iota(jnp.int32, (tq, s_len), 0)).astype(F32)
    tk_pos = lax.broadcasted_iota(jnp.int32, (tq, s_len), 1).astype(F32)
    neg_dist = -jnp.abs(tq_pos - tk_pos)
    for h in range(DIFF_HEADS):
        cols = slice(h * LANES, (h + 1) * LANES)
        q2 = qdf_ref[0, :, cols]
        k2 = kdf_ref[0, :, cols]
        bias = slopes[h] * neg_dist
        q0 = jnp.where(low_half, q2, jnp.zeros_like(q2))
        q1 = jnp.where(low_half, jnp.zeros_like(q2), q2)
        p0, inv0 = _softmax_rows(_dot_nt(q0, k2) + bias)
        p1, inv1 = _softmax_rows(_dot_nt(q1, k2) + bias)
        amat = p0 * inv0 - lam * (p1 * inv1)
        o = _dot(amat.astype(BF16), vdf_ref[0, :, cols])
        o = o * _rms_scale(o) * subg_ref[...] * (1.0 - lam_init)
        ydf_ref[:, cols] = o.astype(BF16)

    ya = _dot(yna_ref[...], wna_ref[...])
    yb = _dot(ydf_ref[...], wdf_ref[...])
    d = ya.shape[1]
    ga = gates_ref[0, :, 0:d].astype(F32)
    gb = gates_ref[0, :, d:2 * d].astype(F32)
    mixed = _dot((ga * ya + gb * yb).astype(BF16), wout_ref[...])
    g1 = mod_ref[0, 2:3, :]
    o_ref[0] = x_ref[0] + g1 * mixed


def _na_bias_table(rpb):
    c = jnp.arange(GRID_W)
    c0 = jnp.clip(c - WIN_COLS // 2, 0, GRID_W - WIN_COLS)
    kcol = jnp.arange(GRID_W)
    valid = (kcol[None, :] >= c0[:, None]) & (kcol[None, :] < c0[:, None] + WIN_COLS)
    dc = jnp.clip(kcol[None, :] - c[:, None] + (WIN_COLS - 1), 0, 2 * WIN_COLS - 2)
    dr = jnp.arange(WIN_ROWS)[:, None] + jnp.arange(WIN_ROWS)[None, :]
    tab = rpb[:, dr[:, None, :, None], dc[None, :, None, :]]
    tab = jnp.where(valid[None, None, :, None, :], tab.astype(F32), MASK_VALUE)
    h = rpb.shape[0]
    return jnp.transpose(tab, (1, 0, 2, 3, 4)).reshape(WIN_ROWS, h, GRID_W, WIN_ROWS * GRID_W)


def _attn(x, mod, proj, gates, na_rpb, lam_vecs, subln_g, w_na_proj, w_df_proj, w_out,
          lam_init, slopes):
    b, s, d = x.shape
    tq = ATTN_TILE
    rows_total = s // GRID_W
    rows_per_tile = tq // GRID_W
    assert rows_per_tile == 4 and rows_total >= WIN_ROWS
    table = _na_bias_table(na_rpb)
    n_heads = na_rpb.shape[0]
    n_keys = WIN_ROWS * GRID_W

    def tab_spec(a):
        def idx(bi, i):
            r = i * rows_per_tile + a
            r0 = jnp.clip(r - WIN_ROWS // 2, 0, rows_total - WIN_ROWS)
            return (r0 - r + WIN_ROWS - 1, 0, 0, 0)
        return pl.BlockSpec((1, n_heads, GRID_W, n_keys), idx)

    qtile = lambda col: pl.BlockSpec((1, tq, GROUP), lambda bi, i: (bi, i, col))
    whole = lambda col: pl.BlockSpec((1, s, GROUP), lambda bi, i: (bi, 0, col))
    const2 = lambda bi, i: (0, 0)
    kern = functools.partial(_attn_kernel, lam_init=lam_init, slopes=slopes,
                             rows_total=rows_total)
    return pl.pallas_call(
        kern,
        grid=(b, s // tq),
        in_specs=[
            qtile(0), whole(1), whole(2), qtile(3), whole(4), whole(5),
            pl.BlockSpec((1, tq, gates.shape[2]), lambda bi, i: (bi, i, 0)),
            pl.BlockSpec((1, tq, d), lambda bi, i: (bi, i, 0)),
            pl.BlockSpec((1, N_MOD, d), lambda bi, i: (bi, 0, 0)),
            tab_spec(0), tab_spec(1), tab_spec(2), tab_spec(3),
            pl.BlockSpec((4, LANES), const2),
            pl.BlockSpec((1, 2 * HEAD_DIM), const2),
            pl.BlockSpec(w_na_proj.shape, const2),
            pl.BlockSpec(w_df_proj.shape, const2),
            pl.BlockSpec(w_out.shape, const2),
        ],
        out_specs=pl.BlockSpec((1, tq, d), lambda bi, i: (bi, i, 0)),
        out_shape=jax.ShapeDtypeStruct((b, s, d), F32),
        scratch_shapes=[pltpu.VMEM((tq, GROUP), BF16), pltpu.VMEM((tq, GROUP), BF16)],
        compiler_params=pltpu.CompilerParams(
            dimension_semantics=("parallel", "arbitrary"),
            vmem_limit_bytes=VMEM_LIMIT_BYTES),
        name="attn",
    )(proj, proj, proj, proj, proj, proj, gates, x, mod, table, table, table, table,
      lam_vecs, subln_g.reshape(1, -1), w_na_proj.astype(BF16), w_df_proj.astype(BF16),
      w_out.astype(BF16))


def _shift_rows(u, direction):
    n = u.shape[0]
    row = lax.broadcasted_iota(jnp.int32, u.shape, 0)
    if direction > 0:
        return jnp.where(row == 0, 0.0, pltpu.roll(u, 1, 0))
    return jnp.where(row == n - 1, 0.0, pltpu.roll(u, n - 1, 0))


def _conv3(u, w_ref, b_ref):
    return (_shift_rows(u, 1) * w_ref[0:1, :] + u * w_ref[1:2, :]
            + _shift_rows(u, -1) * w_ref[2:3, :] + b_ref[...])


def _gelu_erf(x):
    return 0.5 * x * (1.0 + lax.erf(x * math.sqrt(0.5)))


def _ffn_kernel(x_ref, mod_ref, n2g_ref, wa_ref, wl_ref, cwa_ref, cwl_ref, cba_ref, cbl_ref,
                wd_ref, o_ref, h2_ref):
    f = pl.program_id(1)

    @pl.when(f == 0)
    def _():
        x = x_ref[0]
        sh2 = mod_ref[0, 3:4, :]
        sc2 = mod_ref[0, 4:5, :]
        h2 = x * _rms_scale(x) * n2g_ref[...] * (1.0 + sc2) + sh2
        h2_ref[...] = h2.astype(BF16)
        o_ref[0] = jnp.zeros_like(x)

    h2 = h2_ref[...]
    u_act = _conv3(_dot(h2, wa_ref[...]), cwa_ref, cba_ref)
    u_lin = _conv3(_dot(h2, wl_ref[...]), cwl_ref, cbl_ref)
    z = _gelu_erf(u_act) * u_lin
    o_ref[0] += _dot(z.astype(BF16), wd_ref[...])

    @pl.when(f == pl.num_programs(1) - 1)
    def _():
        g2 = mod_ref[0, 5:6, :]
        o_ref[0] = x_ref[0] + g2 * o_ref[0]


def _ffn(x, mod, norm2_g, w_up, conv_w, conv_b, w_down):
    b, s, d = x.shape
    d_ff = w_down.shape[0]
    ck = FFN_CHUNK
    nf = d_ff // ck
    assert nf * ck == d_ff
    w_up_b = w_up.astype(BF16)
    conv_b2 = conv_b.reshape(1, -1)
    const2 = lambda bi, f: (0, 0)
    return pl.pallas_call(
        _ffn_kernel,
        grid=(b, nf),
        in_specs=[
            pl.BlockSpec((1, s, d), lambda bi, f: (bi, 0, 0)),
            pl.BlockSpec((1, N_MOD, d), lambda bi, f: (bi, 0, 0)),
            pl.BlockSpec((1, d), const2),
            pl.BlockSpec((d, ck), lambda bi, f: (0, f)),
            pl.BlockSpec((d, ck), lambda bi, f: (0, f + nf)),
            pl.BlockSpec((CONV_W, ck), lambda bi, f: (0, f)),
            pl.BlockSpec((CONV_W, ck), lambda bi, f: (0, f + nf)),
            pl.BlockSpec((1, ck), lambda bi, f: (0, f)),
            pl.BlockSpec((1, ck), lambda bi, f: (0, f + nf)),
            pl.BlockSpec((ck, d), lambda bi, f: (f, 0)),
        ],
        out_specs=pl.BlockSpec((1, s, d), lambda bi, f: (bi, 0, 0)),
        out_shape=jax.ShapeDtypeStruct((b, s, d), F32),
        scratch_shapes=[pltpu.VMEM((s, d), BF16)],
        compiler_params=pltpu.CompilerParams(
            dimension_semantics=("parallel", "arbitrary"),
            vmem_limit_bytes=VMEM_LIMIT_BYTES),
        name="ffn",
    )(x, mod, norm2_g.reshape(1, d), w_up_b, w_up_b, conv_w, conv_w, conv_b2, conv_b2,
      w_down.astype(BF16))


def _alibi_slopes(n_heads):
    return tuple(2.0 ** (-8.0 * (h + 1) / n_heads) for h in range(n_heads))


def _lambda_init(layer_idx):
    return 0.8 - 0.6 * math.exp(-0.3 * layer_idx)


def kernel(x, c, ada_w, ada_b, norm1_g, w_in, na_q_g, na_k_g, na_rpb, df_q_g, df_k_g, lam_q1,
           lam_k1, lam_q2, lam_k2, df_subln_g, w_na_proj, w_df_proj, w_gate, b_gate, w_out,
           norm2_g, w_up, conv_w, conv_b, w_down):
    b, s, d = x.shape
    depth = ada_w.shape[0]
    slopes = _alibi_slopes(DIFF_HEADS)
    scale = HEAD_DIM ** -0.5
    for l in range(depth):
        lam_init = _lambda_init(l)
        mod = _adaln(c, ada_w[l], ada_b[l]).reshape(b, N_MOD, d)
        tile8 = lambda g: jnp.tile(g.astype(F32), GROUP // HEAD_DIM)
        qk_gain = jnp.stack([tile8(na_q_g[l]) * scale, tile8(na_k_g[l]),
                             tile8(df_q_g[l]) * scale, tile8(df_k_g[l])])
        proj, gates = _proj(x, mod, norm1_g[l], w_in[l], w_gate[l], b_gate[l], qk_gain)
        lam_vecs = jnp.pad(
            jnp.stack([lam_q1[l], lam_k1[l], lam_q2[l], lam_k2[l]]).astype(F32),
            ((0, 0), (0, LANES - HEAD_DIM)))
        x = _attn(x, mod, proj, gates, na_rpb[l], lam_vecs, df_subln_g[l], w_na_proj[l],
                  w_df_proj[l], w_out[l], lam_init, slopes)
        x = _ffn(x, mod, norm2_g[l], w_up[l], conv_w[l], conv_b[l], w_down[l])
    return x
```

```python
import functools
import math

import jax
import jax.numpy as jnp
from jax import lax
from jax.experimental import pallas as pl
from jax.experimental.pallas import tpu as pltpu

HEAD_DIM = 64
NA_HEADS = 8
DIFF_HEADS = 4
GRID_W = 64
WIN_ROWS = 8
WIN_COLS = 16
CONV_W = 3
N_MOD = 6
EPS = 1e-6

LANES = 128
SUBLANES = 8
GROUP = 512
MASK_VALUE = -1e30
LOG2E = math.log2(math.e)
DIAG_OFF = -float(2 ** 14)
FAST_LOGIT_BOUND = 40.0
VMEM_LIMIT_BYTES = 56 * 1024 * 1024

PROJ_TILE = 512
ATTN_TILE = 256
FFN_TILE = 512
FFN_CHUNK = 256

BF16 = jnp.bfloat16
F32 = jnp.float32


def _dot(a, b):
    return jnp.dot(a, b, preferred_element_type=F32)


def _dot_nt(a, b):
    return lax.dot_general(a, b, (((1,), (1,)), ((), ())), preferred_element_type=F32)


def _rms_scale(xf):
    return lax.rsqrt(jnp.mean(xf * xf, axis=-1, keepdims=True) + EPS)


def _adaln_kernel(c_ref, w_ref, b_ref, o_ref):
    o_ref[...] = _dot(jax.nn.silu(c_ref[...]), w_ref[...]) + b_ref[...]


def _adaln(c, ada_w, ada_b):
    b, d = c.shape
    n = ada_w.shape[1]
    return pl.pallas_call(
        _adaln_kernel,
        grid=(n // d,),
        in_specs=[
            pl.BlockSpec((b, d), lambda j: (0, 0)),
            pl.BlockSpec((d, d), lambda j: (0, j)),
            pl.BlockSpec((1, d), lambda j: (0, j)),
        ],
        out_specs=pl.BlockSpec((b, d), lambda j: (0, j)),
        out_shape=jax.ShapeDtypeStruct((b, n), F32),
        compiler_params=pltpu.CompilerParams(dimension_semantics=("arbitrary",)),
        name="adaln",
    )(c, ada_w, ada_b.reshape(1, n))


def _proj_kernel(x_ref, mod_ref, n1g_ref, win_ref, wg_ref, bg_ref, qkg_ref, gsum_ref,
                 proj_ref, gates_ref):
    x = x_ref[0]
    sh1 = mod_ref[0, 0:1, :]
    sc1 = mod_ref[0, 1:2, :]
    h = x * _rms_scale(x) * n1g_ref[...] * (1.0 + sc1) + sh1
    hb = h.astype(BF16)
    normed = {0: 0, 1: 1, 3: 2, 4: 3}
    for j in range(6):
        p = _dot(hb, win_ref[:, j * GROUP:(j + 1) * GROUP])
        if j in normed:
            ms = _dot((p * p).astype(BF16), gsum_ref[...]) * (1.0 / HEAD_DIM)
            g = qkg_ref[normed[j]:normed[j] + 1, :]
            p = p * lax.rsqrt(ms + EPS) * g
        proj_ref[0, :, j * GROUP:(j + 1) * GROUP] = p.astype(BF16)
    g = _dot(hb, wg_ref[...]) + bg_ref[...]
    gates_ref[0] = jax.nn.sigmoid(g).astype(BF16)


def _proj(x, mod, norm1_g, w_in, w_gate, b_gate, qk_gain):
    b, s, d = x.shape
    n_in = w_in.shape[1]
    n_g = w_gate.shape[1]
    tm = PROJ_TILE
    head = jnp.arange(GROUP) // HEAD_DIM
    gsum = (head[:, None] == head[None, :]).astype(BF16)
    const = lambda bi, i: (0, 0)
    return pl.pallas_call(
        _proj_kernel,
        grid=(b, s // tm),
        in_specs=[
            pl.BlockSpec((1, tm, d), lambda bi, i: (bi, i, 0)),
            pl.BlockSpec((1, N_MOD, d), lambda bi, i: (bi, 0, 0)),
            pl.BlockSpec((1, d), const),
            pl.BlockSpec((d, n_in), const),
            pl.BlockSpec((d, n_g), const),
            pl.BlockSpec((1, n_g), const),
            pl.BlockSpec((4, GROUP), const),
            pl.BlockSpec((GROUP, GROUP), const),
        ],
        out_specs=[
            pl.BlockSpec((1, tm, n_in), lambda bi, i: (bi, i, 0)),
            pl.BlockSpec((1, tm, n_g), lambda bi, i: (bi, i, 0)),
        ],
        out_shape=[
            jax.ShapeDtypeStruct((b, s, n_in), BF16),
            jax.ShapeDtypeStruct((b, s, n_g), BF16),
        ],
        compiler_params=pltpu.CompilerParams(
            dimension_semantics=("parallel", "parallel"),
            vmem_limit_bytes=VMEM_LIMIT_BYTES),
        name="proj",
    )(x, mod, norm1_g.reshape(1, d), w_in.astype(BF16), w_gate.astype(BF16),
      b_gate.reshape(1, n_g), qk_gain, gsum)


def _exp2_rows(s, safe):
    if safe:
        s = s - jnp.max(s, axis=-1, keepdims=True)
    p = jnp.exp2(s)
    return p, jnp.sum(p, axis=-1, keepdims=True)


def _na_row(a, i, qna_ref, kna_ref, vna_ref, tab_ref, yna_ref, *, rows_total, safe):
    tq = qna_ref.shape[1]
    n_keys = WIN_ROWS * GRID_W
    low_half = lax.broadcasted_iota(jnp.int32, (1, LANES), 1) < HEAD_DIM
    r = i * (tq // GRID_W) + a
    r0 = jnp.clip(r - WIN_ROWS // 2, 0, rows_total - WIN_ROWS)
    k_start = pl.multiple_of(r0 * GRID_W, GRID_W)
    rows = slice(a * GRID_W, (a + 1) * GRID_W)
    s_parts = []
    for hp in range(NA_HEADS // 2):
        cols = slice(hp * LANES, (hp + 1) * LANES)
        q2 = qna_ref[0, rows, cols]
        k2 = kna_ref[0, pl.ds(k_start, n_keys), cols]
        zero = jnp.zeros_like(q2)
        qs = jnp.concatenate([jnp.where(low_half, q2, zero), jnp.where(low_half, zero, q2)], axis=0)
        s_parts.append(_dot_nt(qs, k2))
    s = jnp.concatenate(s_parts, axis=0) + tab_ref[0]
    p, l = _exp2_rows(s, safe)
    pb = p.astype(BF16)
    inv_l = 1.0 / l
    for hp in range(NA_HEADS // 2):
        cols = slice(hp * LANES, (hp + 1) * LANES)
        prow = slice(2 * hp * GRID_W, (2 * hp + 2) * GRID_W)
        v2 = vna_ref[0, pl.ds(k_start, n_keys), cols]
        o = _dot(pb[prow], v2) * inv_l[prow]
        y = jnp.where(low_half, o[:GRID_W], o[GRID_W:])
        yna_ref[rows, cols] = y.astype(BF16)


def _df_head_fast(h, t0, q2, k2, v2, kdf_ref, vdf_ref, kfeat_ref, qfeat_ref, dbias_ref, lam,
                  low_half):
    tq = q2.shape[0]
    s_len = k2.shape[0]
    cols = slice(h * LANES, (h + 1) * LANES)
    f_start = pl.multiple_of(s_len - tq - t0, tq)
    kf = kfeat_ref[h, pl.ds(f_start, s_len), :]
    kd = kdf_ref[0, pl.ds(t0, tq), cols]
    vd = vdf_ref[0, pl.ds(t0, tq), cols]
    k_aug = jnp.concatenate([k2, kf], axis=1)
    qfeat = qfeat_ref[...]
    parts = []
    for comp in range(2):
        sel = low_half if comp == 0 else jnp.logical_not(low_half)
        qc = jnp.where(sel, q2, jnp.zeros_like(q2))
        p = jnp.exp2(_dot_nt(jnp.concatenate([qc, qfeat], axis=1), k_aug))
        pd = jnp.exp2(_dot_nt(qc, kd) + dbias_ref[h])
        l = jnp.sum(p, axis=-1, keepdims=True) + jnp.sum(pd, axis=-1, keepdims=True)
        parts.append((p, pd, l))
    (p0, pd0, l0), (p1, pd1, l1) = parts
    rho = lam * l0 * (1.0 / l1)
    amat = (p0 - rho * p1).astype(BF16)
    adiag = (pd0 - rho * pd1).astype(BF16)
    return (_dot(amat, v2) + _dot(adiag, vd)) * (1.0 / l0)


def _df_head_safe(neg_dist, q2, k2, v2, lam, low_half, slope2):
    bias = slope2 * neg_dist
    parts = []
    for comp in range(2):
        sel = low_half if comp == 0 else jnp.logical_not(low_half)
        qc = jnp.where(sel, q2, jnp.zeros_like(q2))
        parts.append(_exp2_rows(_dot_nt(qc, k2) + bias, True))
    (p0, l0), (p1, l1) = parts
    amat = p0 * (1.0 / l0) - lam * (p1 * (1.0 / l1))
    return _dot(amat.astype(BF16), v2)


def _attn_kernel(fast_ref, qna_ref, kna_ref, vna_ref, qdf_ref, kdf_ref, vdf_ref, gates_ref, x_ref,
                 mod_ref, tab0_ref, tab1_ref, tab2_ref, tab3_ref, kfeat_ref, qfeat_ref,
                 dbias_ref, lamv_ref, subg_ref, wna_ref, wdf_ref, wout_ref, o_ref,
                 yna_ref, ydf_ref, *, lam_init, slopes2, rows_total):
    i = pl.program_id(1)
    tq = qna_ref.shape[1]
    s_len = kdf_ref.shape[1]
    t0 = pl.multiple_of(i * tq, tq)
    lane = lax.broadcasted_iota(jnp.int32, (1, LANES), 1)
    low_half = lane < HEAD_DIM
    tabs = (tab0_ref, tab1_ref, tab2_ref, tab3_ref)

    lam_t1 = jnp.sum(lamv_ref[0:1, :] * lamv_ref[1:2, :], axis=-1, keepdims=True)
    lam_t2 = jnp.sum(lamv_ref[2:3, :] * lamv_ref[3:4, :], axis=-1, keepdims=True)
    lam = jnp.exp(lam_t1) - jnp.exp(lam_t2) + lam_init

    def finish_head(h, o):
        o = o * _rms_scale(o) * subg_ref[...] * (1.0 - lam_init)
        ydf_ref[:, h * LANES:(h + 1) * LANES] = o.astype(BF16)

    def load_head(h):
        cols = slice(h * LANES, (h + 1) * LANES)
        return qdf_ref[0, :, cols], kdf_ref[0, :, cols], vdf_ref[0, :, cols]

    @pl.when(fast_ref[0] != 0)
    def _():
        for h in range(DIFF_HEADS):
            _na_row(h, i, qna_ref, kna_ref, vna_ref, tabs[h], yna_ref, rows_total=rows_total,
                    safe=False)
            q2, k2, v2 = load_head(h)
            finish_head(h, _df_head_fast(h, t0, q2, k2, v2, kdf_ref, vdf_ref, kfeat_ref,
                                         qfeat_ref, dbias_ref, lam, low_half))

    @pl.when(fast_ref[0] == 0)
    def _():
        for a in range(tq // GRID_W):
            _na_row(a, i, qna_ref, kna_ref, vna_ref, tabs[a], yna_ref, rows_total=rows_total,
                    safe=True)
        tq_pos = (t0 + lax.broadcasted_iota(jnp.int32, (tq, s_len), 0)).astype(F32)
        tk_pos = lax.broadcasted_iota(jnp.int32, (tq, s_len), 1).astype(F32)
        neg_dist = -jnp.abs(tq_pos - tk_pos)
        for h in range(DIFF_HEADS):
            q2, k2, v2 = load_head(h)
            finish_head(h, _df_head_safe(neg_dist, q2, k2, v2, lam, low_half, slopes2[h]))

    ya = _dot(yna_ref[...], wna_ref[...])
    yb = _dot(ydf_ref[...], wdf_ref[...])
    d = ya.shape[1]
    ga = gates_ref[0, :, 0:d].astype(F32)
    gb = gates_ref[0, :, d:2 * d].astype(F32)
    mixed = _dot((ga * ya + gb * yb).astype(BF16), wout_ref[...])
    g1 = mod_ref[0, 2:3, :]
    o_ref[0] = x_ref[0] + g1 * mixed


def _na_bias_table(rpb):
    n_h, n_dr, n_dc = rpb.shape
    half = WIN_COLS - 1
    period = 2 * GRID_W - 1
    fill = jnp.full((n_h, n_dr, period - n_dc), MASK_VALUE, F32)
    w = jnp.concatenate([rpb[..., half:].astype(F32), fill, rpb[..., :half].astype(F32)], axis=-1)
    skew = jnp.tile(w, (1, 1, GRID_W))[..., :GRID_W * (period - 1)]
    toep = skew.reshape(n_h, n_dr, GRID_W, period - 1)[..., :GRID_W]
    c = jnp.arange(GRID_W)
    c0 = jnp.clip(c - WIN_COLS // 2, 0, GRID_W - WIN_COLS)
    kcol = jnp.arange(GRID_W)
    valid = (kcol[None, :] >= c0[:, None]) & (kcol[None, :] < c0[:, None] + WIN_COLS)
    toep = jnp.where(valid[None, None], toep * LOG2E, MASK_VALUE)
    tab = jnp.stack([toep[:, o:o + WIN_ROWS] for o in range(WIN_ROWS)])
    return jnp.transpose(tab, (0, 1, 3, 2, 4)).reshape(WIN_ROWS, n_h * GRID_W, WIN_ROWS * GRID_W)


def _split_bf16(v, n):
    terms = []
    rest = v.astype(F32)
    for _ in range(n):
        t = rest.astype(BF16)
        terms.append(t)
        rest = rest - t.astype(F32)
    return terms


def _alibi_tables(slopes2, tq, s_len):
    m = jnp.arange(tq - s_len, s_len, dtype=jnp.int32)
    left = m < 0
    diag = (m >= 0) & (m < tq)
    sign = jnp.where(left, 1.0, -1.0).astype(F32)
    n = jnp.where(left, m, -m).astype(F32)
    kfeat, dbias = [], []
    r = jnp.arange(tq, dtype=F32)
    for c in slopes2:
        c_terms = _split_bf16(jnp.full(m.shape, c, F32), 2)
        n_terms = _split_bf16(c * n, 3)
        cols = [jnp.where(diag, 0, -sign.astype(BF16) * t) for t in c_terms]
        cols += [jnp.where(diag, DIAG_OFF if j == 0 else 0, t) for j, t in enumerate(n_terms)]
        feat = jnp.stack(cols, axis=1).astype(BF16)
        kfeat.append(jnp.pad(feat, ((0, 0), (0, LANES - feat.shape[1]))))
        dbias.append(-c * jnp.abs(r[:, None] - r[None, :]))
    ones = jnp.ones((tq,), F32)
    qfeat = jnp.stack([r, r, ones, ones, ones], axis=1).astype(BF16)
    qfeat = jnp.pad(qfeat, ((0, 0), (0, LANES - qfeat.shape[1])))
    return jnp.stack(kfeat), qfeat, jnp.stack(dbias)


def _attn(x, mod, proj, gates, na_rpb, lam_vecs, subln_g, w_na_proj, w_df_proj, w_out,
          lam_init, slopes, fast_ok):
    b, s, d = x.shape
    tq = ATTN_TILE
    rows_total = s // GRID_W
    rows_per_tile = tq // GRID_W
    assert rows_per_tile == DIFF_HEADS and rows_total >= WIN_ROWS and tq <= 2 ** 8
    table = _na_bias_table(na_rpb)
    n_heads = na_rpb.shape[0]
    n_keys = WIN_ROWS * GRID_W
    slopes2 = tuple(sl * LOG2E for sl in slopes)
    kfeat, qfeat, dbias = _alibi_tables(slopes2, tq, s)

    def tab_spec(a):
        def idx(bi, i, fast):
            r = i * rows_per_tile + a
            r0 = jnp.clip(r - WIN_ROWS // 2, 0, rows_total - WIN_ROWS)
            return (r0 - r + WIN_ROWS - 1, 0, 0)
        return pl.BlockSpec((1, n_heads * GRID_W, n_keys), idx)

    qtile = lambda col: pl.BlockSpec((1, tq, GROUP), lambda bi, i, fast: (bi, i, col))
    whole = lambda col: pl.BlockSpec((1, s, GROUP), lambda bi, i, fast: (bi, 0, col))
    const2 = lambda bi, i, fast: (0, 0)
    const3 = lambda bi, i, fast: (0, 0, 0)
    kern = functools.partial(_attn_kernel, lam_init=lam_init, slopes2=slopes2,
                             rows_total=rows_total)
    grid_spec = pltpu.PrefetchScalarGridSpec(
        num_scalar_prefetch=1,
        grid=(b, s // tq),
        in_specs=[
            qtile(0), whole(1), whole(2), qtile(3), whole(4), whole(5),
            pl.BlockSpec((1, tq, gates.shape[2]), lambda bi, i, fast: (bi, i, 0)),
            pl.BlockSpec((1, tq, d), lambda bi, i, fast: (bi, i, 0)),
            pl.BlockSpec((1, N_MOD, d), lambda bi, i, fast: (bi, 0, 0)),
            tab_spec(0), tab_spec(1), tab_spec(2), tab_spec(3),
            pl.BlockSpec(kfeat.shape, const3),
            pl.BlockSpec(qfeat.shape, const2),
            pl.BlockSpec(dbias.shape, const3),
            pl.BlockSpec((4, LANES), const2),
            pl.BlockSpec((1, 2 * HEAD_DIM), const2),
            pl.BlockSpec(w_na_proj.shape, const2),
            pl.BlockSpec(w_df_proj.shape, const2),
            pl.BlockSpec(w_out.shape, const2),
        ],
        out_specs=pl.BlockSpec((1, tq, d), lambda bi, i, fast: (bi, i, 0)),
        scratch_shapes=[pltpu.VMEM((tq, GROUP), BF16), pltpu.VMEM((tq, GROUP), BF16)],
    )
    return pl.pallas_call(
        kern,
        grid_spec=grid_spec,
        out_shape=jax.ShapeDtypeStruct((b, s, d), F32),
        compiler_params=pltpu.CompilerParams(
            dimension_semantics=("parallel", "arbitrary"),
            vmem_limit_bytes=VMEM_LIMIT_BYTES),
        name="attn",
    )(fast_ok, proj, proj, proj, proj, proj, proj, gates, x, mod, table, table, table, table,
      kfeat, qfeat, dbias, lam_vecs, subln_g.reshape(1, -1), w_na_proj.astype(BF16),
      w_df_proj.astype(BF16), w_out.astype(BF16))


def _conv3(u_ref, slot, col0, tm, halo, w_ref, b_ref, cols):
    width = cols.stop - cols.start
    view = lambda shift: u_ref[slot, pl.ds(halo + shift, tm), col0:col0 + width]
    return (view(-1) * w_ref[0:1, cols] + view(0) * w_ref[1:2, cols]
            + view(1) * w_ref[2:3, cols] + b_ref[:, cols])


def _gelu_erf(x):
    return 0.5 * x * (1.0 + lax.erf(x * math.sqrt(0.5)))


def _ffn_kernel(x_ref, xprev_ref, xnext_ref, mod_ref, n2g_ref, wup_ref, cw_ref, cb_ref, wd_ref,
                o_ref, u_ref, *, chunk):
    t = pl.program_id(1)
    tm = x_ref.shape[1]
    halo = xprev_ref.shape[1]
    d_ff = wd_ref.shape[0]
    x = x_ref[0]
    xe = jnp.concatenate([xprev_ref[0], x, xnext_ref[0]], axis=0)
    sh2 = mod_ref[0, 3:4, :]
    sc2 = mod_ref[0, 4:5, :]
    h2 = xe * _rms_scale(xe) * n2g_ref[...] * (1.0 + sc2) + sh2
    row = lax.broadcasted_iota(jnp.int32, (xe.shape[0], 1), 0)
    inside = (((row >= halo) | (t > 0))
              & ((row < halo + tm) | (t < pl.num_programs(1) - 1)))
    h2 = jnp.where(inside, h2, 0.0).astype(BF16)
    n_chunks = d_ff // chunk
    acc = jnp.zeros((tm, x.shape[1]), F32)
    u_ref[0] = _dot(h2, wup_ref[:, 0:2 * chunk])
    for f in range(n_chunks):
        slot = f % 2
        if f + 1 < n_chunks:
            u_ref[1 - slot] = _dot(h2, wup_ref[:, 2 * (f + 1) * chunk:2 * (f + 2) * chunk])
        act = slice(2 * f * chunk, (2 * f + 1) * chunk)
        lin = slice((2 * f + 1) * chunk, (2 * f + 2) * chunk)
        u_act = _conv3(u_ref, slot, 0, tm, halo, cw_ref, cb_ref, act)
        u_lin = _conv3(u_ref, slot, chunk, tm, halo, cw_ref, cb_ref, lin)
        z = _gelu_erf(u_act) * u_lin
        acc = acc + _dot(z.astype(BF16), wd_ref[f * chunk:(f + 1) * chunk, :])
    g2 = mod_ref[0, 5:6, :]
    o_ref[0] = x + g2 * acc


def _interleave_chunks(a, chunk):
    lead = a.shape[:-1]
    half = a.shape[-1] // 2
    a = a.reshape(lead + (2, half // chunk, chunk))
    return jnp.swapaxes(a, -3, -2).reshape(lead + (2 * half,))


def _ffn(x, mod, norm2_g, w_up, conv_w, conv_b, w_down):
    b, s, d = x.shape
    d_ff = w_down.shape[0]
    ck = FFN_CHUNK
    tm = FFN_TILE
    halo = SUBLANES
    assert d_ff % ck == 0 and s % tm == 0
    w_up_b = _interleave_chunks(w_up.astype(BF16), ck)
    conv_w2 = _interleave_chunks(conv_w, ck)
    conv_b2 = _interleave_chunks(conv_b.reshape(1, -1), ck)
    const2 = lambda bi, t: (0, 0)
    blocks_per_tile = tm // halo
    last_block = s // halo - 1
    kern = functools.partial(_ffn_kernel, chunk=ck)
    return pl.pallas_call(
        kern,
        grid=(b, s // tm),
        in_specs=[
            pl.BlockSpec((1, tm, d), lambda bi, t: (bi, t, 0)),
            pl.BlockSpec((1, halo, d), lambda bi, t: (bi, jnp.maximum(t * blocks_per_tile - 1, 0), 0)),
            pl.BlockSpec((1, halo, d),
                         lambda bi, t: (bi, jnp.minimum((t + 1) * blocks_per_tile, last_block), 0)),
            pl.BlockSpec((1, N_MOD, d), lambda bi, t: (bi, 0, 0)),
            pl.BlockSpec((1, d), const2),
            pl.BlockSpec(w_up_b.shape, const2),
            pl.BlockSpec(conv_w2.shape, const2),
            pl.BlockSpec(conv_b2.shape, const2),
            pl.BlockSpec(w_down.shape, const2),
        ],
        out_specs=pl.BlockSpec((1, tm, d), lambda bi, t: (bi, t, 0)),
        out_shape=jax.ShapeDtypeStruct((b, s, d), F32),
        scratch_shapes=[pltpu.VMEM((2, tm + 2 * halo, 2 * ck), F32)],
        compiler_params=pltpu.CompilerParams(
            dimension_semantics=("parallel", "parallel"),
            vmem_limit_bytes=VMEM_LIMIT_BYTES),
        name="ffn",
    )(x, x, x, mod, norm2_g.reshape(1, d), w_up_b, conv_w2, conv_b2, w_down.astype(BF16))


def _alibi_slopes(n_heads):
    return tuple(2.0 ** (-8.0 * (h + 1) / n_heads) for h in range(n_heads))


def _lambda_init(layer_idx):
    return 0.8 - 0.6 * math.exp(-0.3 * layer_idx)


def kernel(x, c, ada_w, ada_b, norm1_g, w_in, na_q_g, na_k_g, na_rpb, df_q_g, df_k_g, lam_q1,
           lam_k1, lam_q2, lam_k2, df_subln_g, w_na_proj, w_df_proj, w_gate, b_gate, w_out,
           norm2_g, w_up, conv_w, conv_b, w_down):
    b, s, d = x.shape
    depth = ada_w.shape[0]
    slopes = _alibi_slopes(DIFF_HEADS)
    q_fold = HEAD_DIM ** -0.5 * LOG2E
    amax = lambda g: jnp.max(jnp.abs(g.astype(F32)))
    for l in range(depth):
        lam_init = _lambda_init(l)
        mod = _adaln(c, ada_w[l], ada_b[l]).reshape(b, N_MOD, d)
        tile8 = lambda g: jnp.tile(g.astype(F32), GROUP // HEAD_DIM)
        qk_gain = jnp.stack([tile8(na_q_g[l]) * q_fold, tile8(na_k_g[l]),
                             tile8(df_q_g[l]) * q_fold, tile8(df_k_g[l])])
        proj, gates = _proj(x, mod, norm1_g[l], w_in[l], w_gate[l], b_gate[l], qk_gain)
        bound_na = LOG2E * (HEAD_DIM ** 0.5 * amax(na_q_g[l]) * amax(na_k_g[l]) + amax(na_rpb[l]))
        bound_df = LOG2E * (HEAD_DIM ** 0.5 * amax(df_q_g[l]) * amax(df_k_g[l]))
        fast_ok = (jnp.maximum(bound_na, bound_df) <= FAST_LOGIT_BOUND).astype(jnp.int32).reshape(1)
        lam_vecs = jnp.pad(
            jnp.stack([lam_q1[l], lam_k1[l], lam_q2[l], lam_k2[l]]).astype(F32),
            ((0, 0), (0, LANES - HEAD_DIM)))
        x = _attn(x, mod, proj, gates, na_rpb[l], lam_vecs, df_subln_g[l], w_na_proj[l],
                  w_df_proj[l], w_out[l], lam_init, slopes, fast_ok)
        x = _ffn(x, mod, norm2_g[l], w_up[l], conv_w[l], conv_b[l], w_down[l])
    return x
```
